```python
import math
import jax, jax.numpy as jnp
from jax import lax
import numpy as np


D_MODEL = 1024
BATCH = 8
SEQ = 2048
DEPTH = 2

GRID_W = 64
CTX_LEN = 256
D_MIX = D_MODEL
MLSTM_HEADS = 4
MLSTM_WIDTH = D_MIX // 2
MLSTM_HD = MLSTM_WIDTH // MLSTM_HEADS
MLSTM_CHUNK = 64
SHORT_CONV = 3
N_GATES = 4 * MLSTM_HEADS
FOURIER_HEADS = 4
FOURIER_WIDTH = D_MIX // 4
FOURIER_HD = FOURIER_WIDTH // FOURIER_HEADS
CONV_WIDTH = D_MIX - MLSTM_WIDTH - FOURIER_WIDTH
CONV_K = 31
D_FF = 4 * D_MODEL
Q_OFF = 0
K_OFF = MLSTM_WIDTH
V_OFF = 2 * MLSTM_WIDTH
O_OFF = 3 * MLSTM_WIDTH
G_OFF = 4 * MLSTM_WIDTH
MLSTM_COLS = 4 * MLSTM_WIDTH + N_GATES
F_OFF = MLSTM_COLS
C_OFF = F_OFF + FOURIER_WIDTH
P_IN = C_OFF + 2 * CONV_WIDTH
EPS = 1e-6
POS_BASE = 10000.0

kernel_name = 'hybrid_mlstm_fourier_conv_prefix_block'


def rmsnorm(x, g):
    xf = x.astype(jnp.float32)
    y = xf * lax.rsqrt(jnp.mean(xf * xf, axis=-1, keepdims=True) + EPS)
    return (y * g.astype(jnp.float32)).astype(x.dtype)


def layernorm(x, g, b):
    xf = x.astype(jnp.float32)
    mu = jnp.mean(xf, axis=-1, keepdims=True)
    var = jnp.mean(jnp.square(xf - mu), axis=-1, keepdims=True)
    y = (xf - mu) * lax.rsqrt(var + EPS)
    return (y * g.astype(jnp.float32) + b.astype(jnp.float32)).astype(x.dtype)


def modulate(h, shift, scale):
    return h * (1 + scale) + shift


def grid_sincos(rows, d):
    quarter = d // 4
    freq = jnp.exp(-math.log(POS_BASE) * jnp.arange(quarter, dtype=jnp.float32) / quarter)
    r = jnp.broadcast_to(jnp.arange(rows, dtype=jnp.float32)[:, None], (rows, GRID_W)).reshape(-1)
    col = jnp.broadcast_to(jnp.arange(GRID_W, dtype=jnp.float32)[None, :], (rows, GRID_W)).reshape(-1)
    ar = r[:, None] * freq
    ac = col[:, None] * freq
    return jnp.concatenate([jnp.sin(ar), jnp.cos(ar), jnp.sin(ac), jnp.cos(ac)], axis=-1)


def dwconv(u, w):
    k = w.shape[0]
    ch = u.shape[-1]
    return lax.conv_general_dilated(u, w[:, None, :], window_strides=(1,), padding=[(k // 2, k // 2)],
                                    dimension_numbers=('NWC', 'WIO', 'NWC'), feature_group_count=ch)


def mlstm_chunked(q, k, v, li, lf, state):
    b_, h_, t_, hd = q.shape
    nc = t_ // MLSTM_CHUNK
    L = MLSTM_CHUNK

    def to_chunks(a):
        return jnp.moveaxis(a.reshape(a.shape[:2] + (nc, L) + a.shape[3:]), 2, 0)

    tri = jnp.tril(jnp.ones((L, L), dtype=bool))

    def step(carry, inp):
        C, n, m = carry
        qc, kc, vc, lic, lfc = inp
        bcum = jnp.cumsum(lfc, axis=-1)
        a = bcum + m[..., None]
        D = bcum[..., :, None] - bcum[..., None, :] + lic[..., None, :]
        D = jnp.where(tri, D, -jnp.inf)
        m_t = jnp.maximum(a, jnp.max(D, axis=-1))
        w_inter = jnp.exp(a - m_t)
        S = jnp.einsum('bhld,bhsd->bhls', qc, kc) * jnp.exp(D - m_t[..., None])
        num = w_inter[..., None] * jnp.einsum('bhed,bhld->bhle', C, qc) + jnp.einsum('bhls,bhse->bhle', S, vc)
        den = w_inter * jnp.einsum('bhd,bhld->bhl', n, qc) + jnp.sum(S, axis=-1)
        h = num / jnp.maximum(jnp.abs(den), jnp.exp(-m_t))[..., None]
        bl = bcum[..., -1]
        g = bl[..., None] - bcum + lic
        m_new = jnp.maximum(bl + m, jnp.max(g, axis=-1))
        decay = jnp.exp(bl + m - m_new)
        ws = jnp.exp(g - m_new[..., None])
        C_new = decay[..., None, None] * C + jnp.einsum('bhs,bhse,bhsd->bhed', ws, vc, kc)
        n_new = decay[..., None] * n + jnp.einsum('bhs,bhsd->bhd', ws, kc)
        return (C_new, n_new, m_new), h

    final, hs = lax.scan(step, state, (to_chunks(q), to_chunks(k), to_chunks(v), to_chunks(li), to_chunks(lf)))
    h = jnp.moveaxis(hs, 0, 2).reshape(b_, h_, t_, hd)
    return h, final


def mlstm_heads(z, w_qk_conv, b_gate):
    b_, t_, _ = z.shape
    qk = dwconv(z[..., Q_OFF:V_OFF], w_qk_conv)

    def heads(a):
        return a.reshape(b_, t_, MLSTM_HEADS, MLSTM_HD).transpose(0, 2, 1, 3).astype(jnp.float32)

    q = heads(qk[..., :MLSTM_WIDTH])
    k = heads(qk[..., MLSTM_WIDTH:]) * (MLSTM_HD ** -0.5)
    v = heads(z[..., V_OFF:O_OFF])
    gates = (z[..., G_OFF:G_OFF + N_GATES].reshape(b_, t_, 4, MLSTM_HEADS) + b_gate).astype(jnp.float32)
    gates = gates.transpose(2, 0, 3, 1)
    fwd = (gates[0], jax.nn.log_sigmoid(gates[1]))
    bwd = (gates[2], jax.nn.log_sigmoid(gates[3]))
    return q, k, v, fwd, bwd


def mlstm_out(h, o_pre, g):
    b_, h_, t_, hd = h.shape
    mu = jnp.mean(h, axis=-1, keepdims=True)
    var = jnp.mean(jnp.square(h - mu), axis=-1, keepdims=True)
    hn = ((h - mu) * lax.rsqrt(var + EPS)).transpose(0, 2, 1, 3).reshape(b_, t_, h_ * hd)
    return (hn * g.astype(jnp.float32) * jax.nn.sigmoid(o_pre.astype(jnp.float32))).astype(o_pre.dtype)


def mlstm_mixer(zx, zc, w_qk_conv, b_gate, g_norm, need_ctx):
    qx, kx, vx, fx, bx = mlstm_heads(zx, w_qk_conv, b_gate)
    qc, kc, vc, fc, bc = mlstm_heads(zc, w_qk_conv, b_gate)
    bsz = zx.shape[0]
    zero = (jnp.zeros((bsz, MLSTM_HEADS, MLSTM_HD, MLSTM_HD), jnp.float32),
            jnp.zeros((bsz, MLSTM_HEADS, MLSTM_HD), jnp.float32),
            jnp.zeros((bsz, MLSTM_HEADS), jnp.float32))

    def fl(a):
        return jnp.flip(a, axis=2)

    hc_f, st_f = mlstm_chunked(qc, kc, vc, fc[0], fc[1], zero)
    hx_f, _ = mlstm_chunked(qx, kx, vx, fx[0], fx[1], st_f)
    hc_b, st_b = mlstm_chunked(fl(qc), fl(kc), fl(vc), fl(bc[0]), fl(bc[1]), zero)
    hx_b, _ = mlstm_chunked(fl(qx), fl(kx), fl(vx), fl(bx[0]), fl(bx[1]), st_b)
    yx = mlstm_out(hx_f + fl(hx_b), zx[..., O_OFF:G_OFF], g_norm)
    yc = mlstm_out(hc_f + fl(hc_b), zc[..., O_OFF:G_OFF], g_norm) if need_ctx else None
    return yx, yc


def fourier_mix(u):
    b_, t_, _ = u.shape
    uh = u.reshape(b_, t_, FOURIER_HEADS, FOURIER_HD).astype(jnp.float32)
    y = jnp.real(jnp.fft.fftn(uh, axes=(1, 3), norm='ortho'))
    return y.reshape(b_, t_, FOURIER_WIDTH).astype(u.dtype)


def conv_module(z, w_dw, b_dw, g_ln, b_ln):
    a, gt = z[..., :CONV_WIDTH], z[..., CONV_WIDTH:]
    u = a * jax.nn.sigmoid(gt)
    u = dwconv(u, w_dw) + b_dw
    return jax.nn.silu(layernorm(u, g_ln, b_ln))


def token_mixers(hx, hc, w_in, w_out, b_gate, w_qk_conv, g_mlstm_norm, w_dw, b_dw, g_conv_ln, b_conv_ln, need_ctx):
    zx = hx @ w_in
    zc = hc @ (w_in if need_ctx else w_in[:, :MLSTM_COLS])
    ym_x, ym_c = mlstm_mixer(zx, zc, w_qk_conv, b_gate, g_mlstm_norm, need_ctx)

    def merge(z, ym):
        yf = fourier_mix(z[..., F_OFF:C_OFF])
        yv = conv_module(z[..., C_OFF:P_IN], w_dw, b_dw, g_conv_ln, b_conv_ln)
        return jnp.concatenate([ym, yf, yv], axis=-1) @ w_out

    out_x = merge(zx, ym_x)
    out_c = merge(zc, ym_c) if need_ctx else None
    return out_x, out_c


def sq_relu_mlp(h, w1, w2):
    return jnp.square(jax.nn.relu(h @ w1)) @ w2


def setup_inputs(seed: int = 0) -> dict:
    key = jax.random.key(seed)
    ks = jax.random.split(key, 24)
    f32 = jnp.float32
    L = DEPTH

    def nrm(k, shape, scale):
        return jax.random.normal(k, shape, f32) * scale

    def gain(k, shape):
        return 1.0 + nrm(k, shape, 0.05)

    return {
        'x': nrm(ks[0], (BATCH, SEQ, D_MODEL), 1.0),
        'c': nrm(ks[1], (BATCH, D_MODEL), 1.0),
        'ctx': nrm(ks[2], (BATCH, CTX_LEN, D_MODEL), 1.0),
        'c_ctx': nrm(ks[3], (D_MODEL,), 1.0),
        'w_ada': nrm(ks[4], (L, D_MODEL, 6 * D_MODEL), D_MODEL ** -0.5),
        'b_ada': nrm(ks[5], (L, 6 * D_MODEL), 0.02),
        'g_pre_mix': gain(ks[6], (L, D_MODEL)),
        'g_post_mix': gain(ks[7], (L, D_MODEL)),
        'g_pre_mlp': gain(ks[8], (L, D_MODEL)),
        'g_post_mlp': gain(ks[9], (L, D_MODEL)),
        'w_in': nrm(ks[10], (L, D_MODEL, P_IN), D_MODEL ** -0.5),
        'b_gate': jnp.array([0.0, 3.0, 0.0, 3.0], f32)[None, :, None] + nrm(ks[11], (L, 4, MLSTM_HEADS), 0.1),
        'w_qk_conv': nrm(ks[12], (L, SHORT_CONV, 2 * MLSTM_WIDTH), SHORT_CONV ** -0.5),
        'g_mlstm_norm': gain(ks[13], (L, MLSTM_WIDTH)),
        'w_dw': nrm(ks[14], (L, CONV_K, CONV_WIDTH), CONV_K ** -0.5),
        'b_dw': nrm(ks[15], (L, CONV_WIDTH), 0.02),
        'g_conv_ln': gain(ks[16], (L, CONV_WIDTH)),
        'b_conv_ln': nrm(ks[17], (L, CONV_WIDTH), 0.02),
        'w_out': nrm(ks[18], (L, D_MIX, D_MODEL), D_MIX ** -0.5),
        'w_mlp1': nrm(ks[19], (L, D_MODEL, D_FF), D_MODEL ** -0.5),
        'w_mlp2': nrm(ks[20], (L, D_FF, D_MODEL), D_FF ** -0.5),
    }


def reference(x, c, ctx, c_ctx, w_ada, b_ada, g_pre_mix, g_post_mix, g_pre_mlp, g_post_mlp, w_in, b_gate,
              w_qk_conv, g_mlstm_norm, w_dw, b_dw, g_conv_ln, b_conv_ln, w_out, w_mlp1, w_mlp2):
    n_tok = x.shape[1]
    rows = n_tok // GRID_W
    x = x + grid_sincos(rows, x.shape[-1]).astype(x.dtype)[None]
    for l in range(DEPTH):
        need_ctx = l < DEPTH - 1
        mx = jnp.split((jax.nn.silu(c) @ w_ada[l] + b_ada[l])[:, None, :], 6, axis=-1)
        mc = jnp.split((jax.nn.silu(c_ctx) @ w_ada[l] + b_ada[l])[None, None, :], 6, axis=-1)
        hx = modulate(rmsnorm(x, g_pre_mix[l]), mx[0], mx[1])
        hc = modulate(rmsnorm(ctx, g_pre_mix[l]), mc[0], mc[1])
        out_x, out_c = token_mixers(hx, hc, w_in[l], w_out[l], b_gate[l], w_qk_conv[l], g_mlstm_norm[l],
                                    w_dw[l], b_dw[l], g_conv_ln[l], b_conv_ln[l], need_ctx)
        x = x + mx[2] * rmsnorm(out_x, g_post_mix[l])
        hx = modulate(rmsnorm(x, g_pre_mlp[l]), mx[3], mx[4])
        x = x + mx[5] * rmsnorm(sq_relu_mlp(hx, w_mlp1[l], w_mlp2[l]), g_post_mlp[l])
        if need_ctx:
            ctx = ctx + mc[2] * rmsnorm(out_c, g_post_mix[l])
            hc = modulate(rmsnorm(ctx, g_pre_mlp[l]), mc[3], mc[4])
            ctx = ctx + mc[5] * rmsnorm(sq_relu_mlp(hc, w_mlp1[l], w_mlp2[l]), g_post_mlp[l])
    return x
```

```python
import functools
import math

import jax
import jax.numpy as jnp
from jax import lax
from jax.experimental import pallas as pl
from jax.experimental.pallas import tpu as pltpu

F32 = jnp.float32
BF16 = jnp.bfloat16

D_MODEL = 1024
N_HEADS = 4
HEAD_DIM = 128
MLSTM_W = N_HEADS * HEAD_DIM
FOURIER_W = 256
FOURIER_HD = 64
CONV_W = 256
CONV_K = 31
D_FF = 4096
GRID_W = 64
EPS = 1e-6
POS_BASE = 10000.0

_G_OFF = 4 * MLSTM_W
_F_OFF = _G_OFF + 4 * N_HEADS
_C_OFF = _F_OFF + FOURIER_W
_P_IN = _C_OFF + 2 * CONV_W
_W_QKVO = 4 * MLSTM_W
_W_MAIN = _W_QKVO + FOURIER_W + 2 * CONV_W

CHUNK = 128
GATE_ROWS = 8
ROW_TILE = 512
VMEM_LIMIT = 56 * 1024 * 1024


def _cparams(sem):
    return pltpu.CompilerParams(dimension_semantics=sem, vmem_limit_bytes=VMEM_LIMIT)


def _const_spec(shape):
    nd = len(shape)
    return pl.BlockSpec(shape, lambda *_: (0,) * nd, pipeline_mode=pl.Buffered(1))


def _rms(x, g):
    return x * lax.rsqrt(jnp.mean(x * x, axis=-1, keepdims=True) + EPS) * g


def _sigmoid(x):
    return 1.0 / (1.0 + jnp.exp(-x))


def _dot(a, b):
    return jnp.dot(a, b, preferred_element_type=F32)


def _dot_nt(a, b):
    return lax.dot_general(a, b, (((1,), (1,)), ((), ())), preferred_element_type=F32)


def _dot_tn(a, b):
    return lax.dot_general(a, b, (((0,), (0,)), ((), ())), preferred_element_type=F32)


def _ada_body(c_ref, w_ref, b_ref, o_ref):
    c = c_ref[...]
    s = (c * _sigmoid(c)).astype(BF16)
    o_ref[0] = _dot(s, w_ref[0].astype(BF16)) + b_ref[0]


def _ada(cc, w_ada, b_ada):
    n_l, _, n_out = w_ada.shape
    tn = 1024
    return pl.pallas_call(
        _ada_body,
        grid=(n_l, n_out // tn),
        in_specs=[
            pl.BlockSpec((16, D_MODEL), lambda l, j: (0, 0)),
            pl.BlockSpec((1, D_MODEL, tn), lambda l, j: (l, 0, j)),
            pl.BlockSpec((1, 1, tn), lambda l, j: (l, 0, j)),
        ],
        out_specs=pl.BlockSpec((1, 16, tn), lambda l, j: (l, 0, j)),
        out_shape=jax.ShapeDtypeStruct((n_l, 16, n_out), F32),
        compiler_params=_cparams(("arbitrary", "arbitrary")),
        name="ada",
    )(cc, w_ada, b_ada.reshape(n_l, 1, n_out))


def _in_body(has_pos, *refs):
    if has_pos:
        x_ref, pos_ref, *refs = refs
    else:
        x_ref, *refs = refs
    mod_ref, g_ref, wm_ref, wg_ref, bg_ref, bd_ref, *outs = refs
    if has_pos:
        xn_ref, *outs = outs
    qkvo_ref, u_ref, cz_ref, gr_ref = outs

    x = x_ref[0]
    if has_pos:
        x = x + pos_ref[...]
        xn_ref[0] = x
    h = _rms(x, g_ref[...]) * (1.0 + mod_ref[0, 1:2, :]) + mod_ref[0, 0:1, :]
    hb = h.astype(BF16)

    for c0 in range(0, _W_QKVO, 512):
        qkvo_ref[0, :, c0:c0 + 512] = _dot(hb, wm_ref[:, c0:c0 + 512]).astype(BF16)
    zf = _dot(hb, wm_ref[:, _W_QKVO:_W_QKVO + FOURIER_W]).astype(BF16)
    u_ref[0] = _dot(zf, bd_ref[...]).astype(BF16)
    cz_ref[0] = _dot(hb, wm_ref[:, _W_QKVO + FOURIER_W:]).astype(BF16)

    gt = _dot_nt(wg_ref[...], hb) + bg_ref[...]
    row = lax.broadcasted_iota(jnp.int32, gt.shape, 0)
    lsig = jnp.minimum(gt, 0.0) - jnp.log(1.0 + jnp.exp(-jnp.abs(gt)))
    gt = jnp.where((row & 1) == 1, lsig, gt)
    tm = gt.shape[1]
    for c in range(tm // CHUNK):
        gr_ref[0, :, c] = gt[:, c * CHUNK:(c + 1) * CHUNK].reshape(N_HEADS, GATE_ROWS, CHUNK)


def _in_proj(x, pos, mods, g_pre, wm, wg, bg, bd, tm):
    bk, tk, _ = x.shape
    has_pos = pos is not None
    grid = (bk, tk // tm)
    row_spec = lambda w: pl.BlockSpec((1, tm, w), lambda b, i: (b, i, 0))
    in_specs = [row_spec(D_MODEL)]
    args = [x]
    if has_pos:
        in_specs.append(pl.BlockSpec((tm, D_MODEL), lambda b, i: (i, 0)))
        args.append(pos)
    in_specs += [
        pl.BlockSpec((1, 6, D_MODEL), lambda b, i: (b, 0, 0)),
        _const_spec((1, D_MODEL)),
        _const_spec(wm.shape),
        _const_spec(wg.shape),
        _const_spec(bg.shape),
        _const_spec(bd.shape),
    ]
    args += [mods, g_pre, wm, wg, bg, bd]
    out_specs, out_shape = [], []
    if has_pos:
        out_specs.append(row_spec(D_MODEL))
        out_shape.append(jax.ShapeDtypeStruct((bk, tk, D_MODEL), F32))
    out_specs += [
        row_spec(_W_QKVO), row_spec(2 * FOURIER_W), row_spec(2 * CONV_W),
        pl.BlockSpec((1, N_HEADS, tm // CHUNK, GATE_ROWS, CHUNK), lambda b, i: (b, 0, i, 0, 0)),
    ]
    out_shape += [
        jax.ShapeDtypeStruct((bk, tk, _W_QKVO), BF16),
        jax.ShapeDtypeStruct((bk, tk, 2 * FOURIER_W), BF16),
        jax.ShapeDtypeStruct((bk, tk, 2 * CONV_W), BF16),
        jax.ShapeDtypeStruct((bk, N_HEADS, tk // CHUNK, GATE_ROWS, CHUNK), F32),
    ]
    return pl.pallas_call(
        functools.partial(_in_body, has_pos),
        grid=grid, in_specs=in_specs, out_specs=out_specs, out_shape=out_shape,
        compiler_params=_cparams(("arbitrary", "arbitrary")),
        name="in_proj",
    )(*args)


def _conv3_segment(src_ref, dst_ref, dst_off, w_ref, scale):
    t_len = src_ref.shape[1]
    blk = 256
    w0, w1, w2 = w_ref[0, 0:1, :], w_ref[0, 1:2, :], w_ref[0, 2:3, :]
    row = lax.broadcasted_iota(jnp.int32, (blk, HEAD_DIM), 0)
    zero = jnp.zeros((1, HEAD_DIM), F32)
    for s in range(0, t_len, blk):
        cur = src_ref[0, s:s + blk, :].astype(F32)
        first = src_ref[0, s - 16:s, :].astype(F32)[15:16, :] if s > 0 else zero
        last = src_ref[0, s + blk:s + blk + 16, :].astype(F32)[0:1, :] if s + blk < t_len else zero
        prev = jnp.where(row == 0, first, pltpu.roll(cur, 1, axis=0))
        nxt = jnp.where(row == blk - 1, last, pltpu.roll(cur, blk - 1, axis=0))
        out = w0 * prev + w1 * cur + w2 * nxt
        if scale != 1.0:
            out = out * scale
        dst_ref[dst_off + s:dst_off + s + blk, :] = out.astype(BF16)


def _mlstm_chunk(q, k, vaug, gates, c_ref, m, rev):
    ln = q.shape[0]
    li_r = gates[2:3, :] if rev else gates[0:1, :]
    lf_r = gates[3:4, :] if rev else gates[1:2, :]
    ti = lax.broadcasted_iota(jnp.int32, (ln, ln), 0)
    si = lax.broadcasted_iota(jnp.int32, (ln, ln), 1)
    causal = (si >= ti) if rev else (si <= ti)
    eye = si == ti

    bcum_c = jnp.sum(jnp.where(causal, lf_r, 0.0), axis=1, keepdims=True)
    bcum_r = jnp.sum(jnp.where(eye, bcum_c, 0.0), axis=0, keepdims=True)
    li_c = jnp.sum(jnp.where(eye, li_r, 0.0), axis=1, keepdims=True)
    bl = jnp.sum(lf_r, axis=1, keepdims=True)
    r_r = li_r - bcum_r

    dm = jnp.where(causal, bcum_c + r_r, -jnp.inf)
    a = bcum_c + m
    m_t = jnp.maximum(a, jnp.max(dm, axis=1, keepdims=True))
    w_inter = jnp.exp(a - m_t)
    s = _dot_nt(q, k) * jnp.exp(dm - m_t)
    c_old = c_ref[...]
    hh = w_inter * _dot(q, c_old.astype(BF16)) + _dot(s.astype(BF16), vaug)
    den = hh[:, HEAD_DIM:HEAD_DIM + 1]
    h = hh[:, :HEAD_DIM] / jnp.maximum(jnp.abs(den), jnp.exp(-m_t))

    g_r = bl + r_r
    m_new = jnp.maximum(bl + m, jnp.max(g_r, axis=1, keepdims=True))
    decay = jnp.exp(bl + m - m_new)
    ws_c = jnp.exp(bl - bcum_c + li_c - m_new)
    kw = (k.astype(F32) * ws_c).astype(BF16)
    c_ref[...] = decay * c_old + _dot_tn(kw, vaug)
    return h, m_new


def _mlstm_body(need_ctx, *refs):
    (qx_ref, kx_ref, vx_ref, ox_ref, grx_ref,
     qc_ref, kc_ref, vc_ref, oc_ref, grc_ref,
     wq_ref, wk_ref, gn_ref, *rest) = refs
    if need_ctx:
        yx_ref, yc_ref, *scr = rest
    else:
        yx_ref, *scr = rest
        yc_ref = None
    qs_ref, ks_ref, hfx_ref, hbx_ref, hfc_ref, hbc_ref, cf_ref, cb_ref = scr

    t_x = qx_ref.shape[1]
    t_c = qc_ref.shape[1]
    ln = CHUNK

    _conv3_segment(qx_ref, qs_ref, 0, wq_ref, 1.0)
    _conv3_segment(qc_ref, qs_ref, t_x, wq_ref, 1.0)
    _conv3_segment(kx_ref, ks_ref, 0, wk_ref, HEAD_DIM ** -0.5)
    _conv3_segment(kc_ref, ks_ref, t_x, wk_ref, HEAD_DIM ** -0.5)

    cf_ref[...] = jnp.zeros_like(cf_ref)
    cb_ref[...] = jnp.zeros_like(cb_ref)
    lane = lax.broadcasted_iota(jnp.int32, (ln, HEAD_DIM), 1)
    ones_col = jnp.where(lane == 0, 1.0, 0.0).astype(BF16)

    def segment(v_ref, gr_ref, hf_ref, hb_ref, off, n_chunks, carry):
        def step(i, carry):
            m_f, m_b = carry
            for rev in (False, True):
                c = (n_chunks - 1 - i) if rev else i
                r0 = pl.multiple_of(c * ln, ln)
                q = qs_ref[pl.ds(off + r0, ln), :]
                k = ks_ref[pl.ds(off + r0, ln), :]
                vaug = jnp.concatenate([v_ref[0, pl.ds(r0, ln), :], ones_col], axis=1)
                gates = gr_ref[0, 0, c]
                if rev:
                    h, m_b = _mlstm_chunk(q, k, vaug, gates, cb_ref, m_b, True)
                    hb_ref[pl.ds(r0, ln), :] = h
                else:
                    h, m_f = _mlstm_chunk(q, k, vaug, gates, cf_ref, m_f, False)
                    hf_ref[pl.ds(r0, ln), :] = h
            return m_f, m_b
        return lax.fori_loop(0, n_chunks, step, carry)

    zero = jnp.zeros((1, 1), F32)
    carry = segment(vc_ref, grc_ref, hfc_ref, hbc_ref, t_x, t_c // ln, (zero, zero))
    segment(vx_ref, grx_ref, hfx_ref, hbx_ref, 0, t_x // ln, carry)

    def finish(hf_ref, hb_ref, o_ref, y_ref):
        blk = 256
        for s in range(0, hf_ref.shape[0], blk):
            h = hf_ref[s:s + blk, :] + hb_ref[s:s + blk, :]
            mu = jnp.mean(h, axis=-1, keepdims=True)
            hc = h - mu
            var = jnp.mean(hc * hc, axis=-1, keepdims=True)
            hn = hc * lax.rsqrt(var + EPS)
            o = o_ref[0, s:s + blk, :].astype(F32)
            y_ref[0, s:s + blk, :] = (hn * gn_ref[0] * _sigmoid(o)).astype(BF16)

    finish(hfx_ref, hbx_ref, ox_ref, yx_ref)
    if need_ctx:
        finish(hfc_ref, hbc_ref, oc_ref, yc_ref)


def _mlstm(qkvo_x, gr_x, qkvo_c, gr_c, wq, wk, gn, need_ctx):
    bsz, t_x, _ = qkvo_x.shape
    t_c = qkvo_c.shape[1]

    def col_spec(t_len, base):
        return pl.BlockSpec((1, t_len, HEAD_DIM), lambda b, h: (b, 0, base + h))

    def gate_spec(t_len):
        return pl.BlockSpec((1, 1, t_len // CHUNK, GATE_ROWS, CHUNK), lambda b, h: (b, h, 0, 0, 0))

    head_spec = lambda rows: pl.BlockSpec((1, rows, HEAD_DIM), lambda b, h: (h, 0, 0))
    in_specs = (
        [col_spec(t_x, j * N_HEADS) for j in range(4)] + [gate_spec(t_x)]
        + [col_spec(t_c, j * N_HEADS) for j in range(4)] + [gate_spec(t_c)]
        + [head_spec(3), head_spec(3), head_spec(1)]
    )
    out_specs = [pl.BlockSpec((1, t_x, HEAD_DIM), lambda b, h: (b, 0, h))]
    out_shape = [jax.ShapeDtypeStruct((bsz, t_x, MLSTM_W), BF16)]
    if need_ctx:
        out_specs.append(pl.BlockSpec((1, t_c, HEAD_DIM), lambda b, h: (b, 0, h)))
        out_shape.append(jax.ShapeDtypeStruct((bsz, t_c, MLSTM_W), BF16))
    scratch = [
        pltpu.VMEM((t_x + t_c, HEAD_DIM), BF16), pltpu.VMEM((t_x + t_c, HEAD_DIM), BF16),
        pltpu.VMEM((t_x, HEAD_DIM), F32), pltpu.VMEM((t_x, HEAD_DIM), F32),
        pltpu.VMEM((t_c, HEAD_DIM), F32), pltpu.VMEM((t_c, HEAD_DIM), F32),
        pltpu.VMEM((HEAD_DIM, 2 * HEAD_DIM), F32), pltpu.VMEM((HEAD_DIM, 2 * HEAD_DIM), F32),
    ]
    outs = pl.pallas_call(
        functools.partial(_mlstm_body, need_ctx),
        grid=(bsz, N_HEADS),
        in_specs=in_specs, out_specs=out_specs, out_shape=out_shape, scratch_shapes=scratch,
        compiler_params=_cparams(("arbitrary", "arbitrary")),
        name="mlstm",
    )(qkvo_x, qkvo_x, qkvo_x, qkvo_x, gr_x, qkvo_c, qkvo_c, qkvo_c, qkvo_c, gr_c, wq, wk, gn)
    return (outs[0], outs[1]) if need_ctx else (outs[0], None)


def _dft_body(mc_ref, ms_ref, u_ref, y_ref):
    y = _dot(mc_ref[...], u_ref[0, :, :FOURIER_W]) + _dot(ms_ref[...], u_ref[0, :, FOURIER_W:])
    y_ref[0] = y.astype(BF16)


def _time_dft(u, mc, ms):
    bsz, t_len, _ = u.shape
    tm = min(t_len, 1024)
    return pl.pallas_call(
        _dft_body,
        grid=(t_len // tm, bsz),
        in_specs=[
            pl.BlockSpec((tm, t_len), lambda i, b: (i, 0)),
            pl.BlockSpec((tm, t_len), lambda i, b: (i, 0)),
            pl.BlockSpec((1, t_len, 2 * FOURIER_W), lambda i, b: (b, 0, 0)),
        ],
        out_specs=pl.BlockSpec((1, tm, FOURIER_W), lambda i, b: (b, i, 0)),
        out_shape=jax.ShapeDtypeStruct((bsz, t_len, FOURIER_W), BF16),
        compiler_params=_cparams(("arbitrary", "arbitrary")),
        name="time_dft",
    )(mc, ms, u)


_CONV_PAD = 16
_CONV_BLK = 128


def _conv_body(cz_ref, w_ref, b_ref, g_ref, bl_ref, y_ref, sh_ref):
    t_len = cz_ref.shape[1]
    n_sh = sh_ref.shape[1]
    sh_ref[0, 0:_CONV_PAD, :] = jnp.zeros((_CONV_PAD, CONV_W), F32)
    sh_ref[0, _CONV_PAD + t_len:n_sh, :] = jnp.zeros((n_sh - _CONV_PAD - t_len, CONV_W), F32)
    blk = 256
    for s in range(0, t_len, blk):
        a = cz_ref[0, s:s + blk, :CONV_W].astype(F32)
        gt = cz_ref[0, s:s + blk, CONV_W:].astype(F32)
        sh_ref[0, _CONV_PAD + s:_CONV_PAD + s + blk, :] = a * _sigmoid(gt)
    n_copy = n_sh - 8
    for r in range(1, 8):
        for s in range(0, n_copy, blk):
            n = min(blk, n_copy - s)
            sh_ref[r, s:s + n, :] = sh_ref[0, s + r:s + r + n, :]

    def block(i, _):
        t0 = pl.multiple_of(i * _CONV_BLK, _CONV_BLK)
        acc = jnp.zeros((_CONV_BLK, CONV_W), F32)
        for k in range(CONV_K):
            j = k + _CONV_PAD - CONV_K // 2
            acc = acc + w_ref[k:k + 1, :] * sh_ref[j % 8, pl.ds(t0 + (j // 8) * 8, _CONV_BLK), :]
        u = acc + b_ref[...]
        mu = jnp.mean(u, axis=-1, keepdims=True)
        uc = u - mu
        var = jnp.mean(uc * uc, axis=-1, keepdims=True)
        v = uc * lax.rsqrt(var + EPS) * g_ref[...] + bl_ref[...]
        y_ref[0, pl.ds(t0, _CONV_BLK), :] = (v * _sigmoid(v)).astype(BF16)
        return 0

    lax.fori_loop(0, t_len // _CONV_BLK, block, 0)


def _conv_module(cz, w_dw, b_dw, g_ln, b_ln):
    bsz, t_len, _ = cz.shape
    n_sh = t_len + 2 * _CONV_PAD + 8
    return pl.pallas_call(
        _conv_body,
        grid=(bsz,),
        in_specs=[
            pl.BlockSpec((1, t_len, 2 * CONV_W), lambda b: (b, 0, 0)),
            _const_spec((CONV_K, CONV_W)),
            _const_spec((1, CONV_W)), _const_spec((1, CONV_W)), _const_spec((1, CONV_W)),
        ],
        out_specs=pl.BlockSpec((1, t_len, CONV_W), lambda b: (b, 0, 0)),
        out_shape=jax.ShapeDtypeStruct((bsz, t_len, CONV_W), BF16),
        scratch_shapes=[pltpu.VMEM((8, n_sh, CONV_W), F32)],
        compiler_params=_cparams(("arbitrary",)),
        name="conv_module",
    )(cz, w_dw, b_dw, g_ln, b_ln)


_FF_BLK = 1024


def _post_body(x_ref, ym_ref, yf_ref, yv_ref, mod_ref, gpm_ref, gqm_ref, gpl_ref,
               wo_ref, w1_ref, w2_ref, o_ref):
    x = x_ref[0]
    o = (_dot(ym_ref[0], wo_ref[0:MLSTM_W, :])
         + _dot(yf_ref[0], wo_ref[MLSTM_W:MLSTM_W + FOURIER_W, :])
         + _dot(yv_ref[0], wo_ref[MLSTM_W + FOURIER_W:, :]))
    x = x + mod_ref[0, 2:3, :] * _rms(o, gpm_ref[...])
    h = (_rms(x, gqm_ref[...]) * (1.0 + mod_ref[0, 4:5, :]) + mod_ref[0, 3:4, :]).astype(BF16)
    acc = jnp.zeros(x.shape, F32)
    for c0 in range(0, D_FF, _FF_BLK):
        a = jnp.maximum(_dot(h, w1_ref[:, c0:c0 + _FF_BLK]), 0.0)
        acc = acc + _dot((a * a).astype(BF16), w2_ref[c0:c0 + _FF_BLK, :])
    o_ref[0] = x + mod_ref[0, 5:6, :] * _rms(acc, gpl_ref[...])


def _post(x, ym, yf, yv, mods, g_post_mix, g_pre_mlp, g_post_mlp, wo, w1, w2, tm):
    bk, tk, _ = x.shape
    row_spec = lambda w: pl.BlockSpec((1, tm, w), lambda b, i: (b, i, 0))
    return pl.pallas_call(
        _post_body,
        grid=(bk, tk // tm),
        in_specs=[
            row_spec(D_MODEL), row_spec(MLSTM_W), row_spec(FOURIER_W), row_spec(CONV_W),
            pl.BlockSpec((1, 6, D_MODEL), lambda b, i: (b, 0, 0)),
            _const_spec((1, D_MODEL)), _const_spec((1, D_MODEL)), _const_spec((1, D_MODEL)),
            _const_spec(wo.shape), _const_spec(w1.shape), _const_spec(w2.shape),
        ],
        out_specs=row_spec(D_MODEL),
        out_shape=jax.ShapeDtypeStruct((bk, tk, D_MODEL), F32),
        compiler_params=_cparams(("arbitrary", "arbitrary")),
        name="post_mlp",
    )(x, ym, yf, yv, mods, g_post_mix, g_pre_mlp, g_post_mlp, wo, w1, w2)


def _grid_sincos(rows, d):
    quarter = d // 4
    freq = jnp.exp(-math.log(POS_BASE) * jnp.arange(quarter, dtype=F32) / quarter)
    r = jnp.broadcast_to(jnp.arange(rows, dtype=F32)[:, None], (rows, GRID_W)).reshape(-1)
    col = jnp.broadcast_to(jnp.arange(GRID_W, dtype=F32)[None, :], (rows, GRID_W)).reshape(-1)
    ar = r[:, None] * freq
    ac = col[:, None] * freq
    return jnp.concatenate([jnp.sin(ar), jnp.cos(ar), jnp.sin(ac), jnp.cos(ac)], axis=-1)


def _dft_tables(n, scale):
    t = jnp.arange(n, dtype=jnp.int32)
    ang = ((t[:, None] * t[None, :]) % n).astype(F32) * (2.0 * math.pi / n)
    return (jnp.cos(ang) * scale).astype(BF16), (-jnp.sin(ang) * scale).astype(BF16)


def _channel_dft_table():
    c = jnp.arange(FOURIER_W, dtype=jnp.int32)
    same = (c[:, None] // FOURIER_HD) == (c[None, :] // FOURIER_HD)
    ang = (((c[:, None] % FOURIER_HD) * (c[None, :] % FOURIER_HD)) % FOURIER_HD).astype(F32) * (
        2.0 * math.pi / FOURIER_HD)
    cos = jnp.where(same, jnp.cos(ang), 0.0)
    sin = jnp.where(same, jnp.sin(ang), 0.0)
    return jnp.concatenate([cos, sin], axis=1).astype(BF16)


def _pack_in_weights(w_in, b_gate):
    wm = jnp.concatenate([w_in[:, :_W_QKVO], w_in[:, _F_OFF:_P_IN]], axis=1).astype(BF16)
    wg = w_in[:, _G_OFF:_F_OFF].reshape(D_MODEL, 4, N_HEADS).transpose(2, 1, 0)
    wg = jnp.pad(wg, ((0, 0), (0, GATE_ROWS - 4), (0, 0))).reshape(N_HEADS * GATE_ROWS, D_MODEL).astype(BF16)
    bg = jnp.pad(b_gate.T, ((0, 0), (0, GATE_ROWS - 4))).reshape(N_HEADS * GATE_ROWS, 1)
    return wm, wg, bg


def kernel(x, c, ctx, c_ctx, w_ada, b_ada, g_pre_mix, g_post_mix, g_pre_mlp, g_post_mlp, w_in, b_gate,
           w_qk_conv, g_mlstm_norm, w_dw, b_dw, g_conv_ln, b_conv_ln, w_out, w_mlp1, w_mlp2):
    bsz, t_x, d = x.shape
    t_c = ctx.shape[1]
    depth = w_ada.shape[0]

    pos = _grid_sincos(t_x // GRID_W, d)
    bd = _channel_dft_table()
    mcx, msx = _dft_tables(t_x, 1.0 / math.sqrt(t_x * FOURIER_HD))
    mcc, msc = _dft_tables(t_c, 1.0 / math.sqrt(t_c * FOURIER_HD))

    cc = jnp.zeros((16, d), F32).at[:bsz].set(c).at[bsz].set(c_ctx)
    mods = _ada(cc, w_ada, b_ada).reshape(depth, 16, 6, d)

    xs = x
    cs = ctx.reshape(1, bsz * t_c, d)
    for l in range(depth):
        need_ctx = l < depth - 1
        mods_x = mods[l, :bsz]
        mods_c = mods[l, bsz:bsz + 1]
        wm, wg, bg = _pack_in_weights(w_in[l], b_gate[l])
        row = lambda v: v[l].reshape(1, -1)

        outs = _in_proj(xs, pos if l == 0 else None, mods_x, row(g_pre_mix), wm, wg, bg, bd, ROW_TILE)
        if l == 0:
            xs, *outs = outs
        qkvo_x, u_x, cz_x, gr_x = outs
        qkvo_c, u_c, cz_c, gr_c = _in_proj(cs, None, mods_c, row(g_pre_mix), wm, wg, bg, bd, ROW_TILE)
        per_sample = lambda a: a.reshape((bsz, t_c) + a.shape[2:])
        qkvo_c, u_c, cz_c = per_sample(qkvo_c), per_sample(u_c), per_sample(cz_c)
        gr_c = gr_c.reshape(N_HEADS, bsz, t_c // CHUNK, GATE_ROWS, CHUNK).transpose(1, 0, 2, 3, 4)

        wqk = w_qk_conv[l].reshape(3, 2 * N_HEADS, HEAD_DIM).transpose(1, 0, 2)
        gn = g_mlstm_norm[l].reshape(N_HEADS, 1, HEAD_DIM)
        ym_x, ym_c = _mlstm(qkvo_x, gr_x, qkvo_c, gr_c, wqk[:N_HEADS], wqk[N_HEADS:], gn, need_ctx)

        conv_args = (w_dw[l], row(b_dw), row(g_conv_ln), row(b_conv_ln))
        yf_x = _time_dft(u_x, mcx, msx)
        yv_x = _conv_module(cz_x, *conv_args)
        post_w = (row(g_post_mix), row(g_pre_mlp), row(g_post_mlp),
                  w_out[l].astype(BF16), w_mlp1[l].astype(BF16), w_mlp2[l].astype(BF16))
        xs = _post(xs, ym_x, yf_x, yv_x, mods_x, *post_w, ROW_TILE)
        if need_ctx:
            yf_c = _time_dft(u_c, mcc, msc)
            yv_c = _conv_module(cz_c, *conv_args)
            flat = lambda a: a.reshape(1, bsz * t_c, a.shape[-1])
            cs = _post(cs, flat(ym_c), flat(yf_c), flat(yv_c), mods_c, *post_w, ROW_TILE)
    return xs
```

```python
import functools
import math

import jax
import jax.numpy as jnp
from jax import lax
from jax.experimental import pallas as pl
from jax.experimental.pallas import tpu as pltpu

F32 = jnp.float32
BF16 = jnp.bfloat16

D_MODEL = 1024
N_HEADS = 4
HEAD_DIM = 128
MLSTM_W = N_HEADS * HEAD_DIM
FOURIER_W = 256
FOURIER_HD = 64
CONV_W = 256
CONV_K = 31
D_FF = 4096
GRID_W = 64
EPS = 1e-6
POS_BASE = 10000.0

_G_OFF = 4 * MLSTM_W
_F_OFF = _G_OFF + 4 * N_HEADS
_C_OFF = _F_OFF + FOURIER_W
_P_IN = _C_OFF + 2 * CONV_W
_W_QKVO = 4 * MLSTM_W
_W_MAIN = _W_QKVO + FOURIER_W + 2 * CONV_W

CHUNK = 128
GATE_ROWS = 8
ROW_TILE = 512
VMEM_LIMIT = 56 * 1024 * 1024


def _cparams(sem):
    return pltpu.CompilerParams(dimension_semantics=sem, vmem_limit_bytes=VMEM_LIMIT)


def _const_spec(shape):
    nd = len(shape)
    return pl.BlockSpec(shape, lambda *_: (0,) * nd, pipeline_mode=pl.Buffered(1))


def _rms(x, g):
    return x * lax.rsqrt(jnp.mean(x * x, axis=-1, keepdims=True) + EPS) * g


def _sigmoid(x):
    return 1.0 / (1.0 + jnp.exp(-x))


def _dot(a, b):
    return jnp.dot(a, b, preferred_element_type=F32)


def _dot_nt(a, b):
    return lax.dot_general(a, b, (((1,), (1,)), ((), ())), preferred_element_type=F32)


def _dot_tn(a, b):
    return lax.dot_general(a, b, (((0,), (0,)), ((), ())), preferred_element_type=F32)


def _ada_body(c_ref, w_ref, b_ref, o_ref):
    c = c_ref[...]
    s = (c * _sigmoid(c)).astype(BF16)
    o_ref[0] = _dot(s, w_ref[0].astype(BF16)) + b_ref[0]


def _ada(cc, w_ada, b_ada):
    n_l, _, n_out = w_ada.shape
    tn = 1024
    return pl.pallas_call(
        _ada_body,
        grid=(n_l, n_out // tn),
        in_specs=[
            pl.BlockSpec((16, D_MODEL), lambda l, j: (0, 0)),
            pl.BlockSpec((1, D_MODEL, tn), lambda l, j: (l, 0, j)),
            pl.BlockSpec((1, 1, tn), lambda l, j: (l, 0, j)),
        ],
        out_specs=pl.BlockSpec((1, 16, tn), lambda l, j: (l, 0, j)),
        out_shape=jax.ShapeDtypeStruct((n_l, 16, n_out), F32),
        compiler_params=_cparams(("arbitrary", "arbitrary")),
        name="ada",
    )(cc, w_ada, b_ada.reshape(n_l, 1, n_out))


def _in_body(has_pos, *refs):
    if has_pos:
        x_ref, pos_ref, *refs = refs
    else:
        x_ref, *refs = refs
    mod_ref, g_ref, wm_ref, wg_ref, bg_ref, bd_ref, *outs = refs
    if has_pos:
        xn_ref, *outs = outs
    qkvo_ref, u_ref, cz_ref, gr_ref = outs

    x = x_ref[0]
    if has_pos:
        x = x + pos_ref[...]
        xn_ref[0] = x
    h = _rms(x, g_ref[...]) * (1.0 + mod_ref[0, 1:2, :]) + mod_ref[0, 0:1, :]
    hb = h.astype(BF16)

    for c0 in range(0, _W_QKVO, 512):
        qkvo_ref[0, :, c0:c0 + 512] = _dot(hb, wm_ref[:, c0:c0 + 512]).astype(BF16)
    zf = _dot(hb, wm_ref[:, _W_QKVO:_W_QKVO + FOURIER_W]).astype(BF16)
    u_ref[0] = _dot(zf, bd_ref[...]).astype(BF16)
    cz_ref[0] = _dot(hb, wm_ref[:, _W_QKVO + FOURIER_W:]).astype(BF16)

    gt = _dot_nt(wg_ref[...], hb) + bg_ref[...]
    row = lax.broadcasted_iota(jnp.int32, gt.shape, 0)
    lsig = jnp.minimum(gt, 0.0) - jnp.log(1.0 + jnp.exp(-jnp.abs(gt)))
    gt = jnp.where((row & 1) == 1, lsig, gt)
    tm = gt.shape[1]
    for c in range(tm // CHUNK):
        gr_ref[0, :, c] = gt[:, c * CHUNK:(c + 1) * CHUNK].reshape(N_HEADS, GATE_ROWS, CHUNK)


def _in_proj(x, pos, mods, g_pre, wm, wg, bg, bd, tm):
    bk, tk, _ = x.shape
    has_pos = pos is not None
    grid = (bk, tk // tm)
    row_spec = lambda w: pl.BlockSpec((1, tm, w), lambda b, i: (b, i, 0))
    in_specs = [row_spec(D_MODEL)]
    args = [x]
    if has_pos:
        in_specs.append(pl.BlockSpec((tm, D_MODEL), lambda b, i: (i, 0)))
        args.append(pos)
    in_specs += [
        pl.BlockSpec((1, 6, D_MODEL), lambda b, i: (b, 0, 0)),
        _const_spec((1, D_MODEL)),
        _const_spec(wm.shape),
        _const_spec(wg.shape),
        _const_spec(bg.shape),
        _const_spec(bd.shape),
    ]
    args += [mods, g_pre, wm, wg, bg, bd]
    out_specs, out_shape = [], []
    if has_pos:
        out_specs.append(row_spec(D_MODEL))
        out_shape.append(jax.ShapeDtypeStruct((bk, tk, D_MODEL), F32))
    out_specs += [
        row_spec(_W_QKVO), row_spec(2 * FOURIER_W), row_spec(2 * CONV_W),
        pl.BlockSpec((1, N_HEADS, tm // CHUNK, GATE_ROWS, CHUNK), lambda b, i: (b, 0, i, 0, 0)),
    ]
    out_shape += [
        jax.ShapeDtypeStruct((bk, tk, _W_QKVO), BF16),
        jax.ShapeDtypeStruct((bk, tk, 2 * FOURIER_W), BF16),
        jax.ShapeDtypeStruct((bk, tk, 2 * CONV_W), BF16),
        jax.ShapeDtypeStruct((bk, N_HEADS, tk // CHUNK, GATE_ROWS, CHUNK), F32),
    ]
    return pl.pallas_call(
        functools.partial(_in_body, has_pos),
        grid=grid, in_specs=in_specs, out_specs=out_specs, out_shape=out_shape,
        compiler_params=_cparams(("arbitrary", "arbitrary")),
        name="in_proj",
    )(*args)


def _conv3_segment(src_ref, col0, dst_ref, dst_off, dst_col0, w_ref, w_row, scale):
    t_len = src_ref.shape[1]
    blk = 256
    cols = slice(col0, col0 + HEAD_DIM)
    w0, w1, w2 = w_ref[w_row, 0:1, :], w_ref[w_row, 1:2, :], w_ref[w_row, 2:3, :]
    row = lax.broadcasted_iota(jnp.int32, (blk, HEAD_DIM), 0)
    zero = jnp.zeros((1, HEAD_DIM), F32)
    for s in range(0, t_len, blk):
        cur = src_ref[0, s:s + blk, cols].astype(F32)
        first = src_ref[0, s - 16:s, cols].astype(F32)[15:16, :] if s > 0 else zero
        last = src_ref[0, s + blk:s + blk + 16, cols].astype(F32)[0:1, :] if s + blk < t_len else zero
        prev = jnp.where(row == 0, first, pltpu.roll(cur, 1, axis=0))
        nxt = jnp.where(row == blk - 1, last, pltpu.roll(cur, blk - 1, axis=0))
        out = w0 * prev + w1 * cur + w2 * nxt
        if scale != 1.0:
            out = out * scale
        dst_ref[dst_off + s:dst_off + s + blk, dst_col0:dst_col0 + HEAD_DIM] = out.astype(BF16)


def _mlstm_chunk(q, k, vaug, gates, c_ref, c_idx, m, rev):
    ln = q.shape[0]
    li_r = gates[2:3, :] if rev else gates[0:1, :]
    lf_r = gates[3:4, :] if rev else gates[1:2, :]
    ti = lax.broadcasted_iota(jnp.int32, (ln, ln), 0)
    si = lax.broadcasted_iota(jnp.int32, (ln, ln), 1)
    causal = (si >= ti) if rev else (si <= ti)
    eye = si == ti

    bcum_c = jnp.sum(jnp.where(causal, lf_r, 0.0), axis=1, keepdims=True)
    bcum_r = jnp.sum(jnp.where(eye, bcum_c, 0.0), axis=0, keepdims=True)
    li_c = jnp.sum(jnp.where(eye, li_r, 0.0), axis=1, keepdims=True)
    bl = jnp.sum(lf_r, axis=1, keepdims=True)
    r_r = li_r - bcum_r

    dm = jnp.where(causal, bcum_c + r_r, -jnp.inf)
    a = bcum_c + m
    m_t = jnp.maximum(a, jnp.max(dm, axis=1, keepdims=True))
    w_inter = jnp.exp(a - m_t)
    s = _dot_nt(q, k) * jnp.exp(dm - m_t)
    c_old = c_ref[c_idx]
    hh = w_inter * _dot(q, c_old.astype(BF16)) + _dot(s.astype(BF16), vaug)
    den = hh[:, HEAD_DIM:HEAD_DIM + 1]
    h = hh[:, :HEAD_DIM] / jnp.maximum(jnp.abs(den), jnp.exp(-m_t))

    g_r = bl + r_r
    m_new = jnp.maximum(bl + m, jnp.max(g_r, axis=1, keepdims=True))
    decay = jnp.exp(bl + m - m_new)
    ws_c = jnp.exp(bl - bcum_c + li_c - m_new)
    kw = (k.astype(F32) * ws_c).astype(BF16)
    c_ref[c_idx] = decay * c_old + _dot_tn(kw, vaug)
    return h, m_new


def _mlstm_body(need_ctx, *refs):
    zx_ref, grx_ref, zc_ref, grc_ref, wqk_ref, gn_ref, *rest = refs
    if need_ctx:
        yx_ref, yc_ref, *scr = rest
    else:
        yx_ref, *scr = rest
        yc_ref = None
    qs_ref, ks_ref, hfx_ref, hbx_ref, hfc_ref, hbc_ref, st_ref = scr

    t_x = zx_ref.shape[1]
    t_c = zc_ref.shape[1]
    ln = CHUNK
    q0, k0, v0, o0 = 0, MLSTM_W, 2 * MLSTM_W, 3 * MLSTM_W

    for h in range(N_HEADS):
        hc = h * HEAD_DIM
        _conv3_segment(zx_ref, q0 + hc, qs_ref, 0, hc, wqk_ref, h, 1.0)
        _conv3_segment(zc_ref, q0 + hc, qs_ref, t_x, hc, wqk_ref, h, 1.0)
        _conv3_segment(zx_ref, k0 + hc, ks_ref, 0, hc, wqk_ref, N_HEADS + h, HEAD_DIM ** -0.5)
        _conv3_segment(zc_ref, k0 + hc, ks_ref, t_x, hc, wqk_ref, N_HEADS + h, HEAD_DIM ** -0.5)

    st_ref[...] = jnp.zeros_like(st_ref)
    lane = lax.broadcasted_iota(jnp.int32, (ln, HEAD_DIM), 1)
    ones_col = jnp.where(lane == 0, 1.0, 0.0).astype(BF16)

    def segment(z_ref, gr_ref, hf_ref, hb_ref, off, n_chunks, carry):
        def step(i, carry):
            ms = list(carry)
            for h in range(N_HEADS):
                hc = slice(h * HEAD_DIM, (h + 1) * HEAD_DIM)
                for rev in (False, True):
                    c = (n_chunks - 1 - i) if rev else i
                    r0 = pl.multiple_of(c * ln, ln)
                    q = qs_ref[pl.ds(off + r0, ln), hc]
                    k = ks_ref[pl.ds(off + r0, ln), hc]
                    v = z_ref[0, pl.ds(r0, ln), v0 + h * HEAD_DIM:v0 + (h + 1) * HEAD_DIM]
                    vaug = jnp.concatenate([v, ones_col], axis=1)
                    idx = 2 * h + int(rev)
                    hh, ms[idx] = _mlstm_chunk(q, k, vaug, gr_ref[0, h, c], st_ref, idx, ms[idx], rev)
                    (hb_ref if rev else hf_ref)[pl.ds(r0, ln), hc] = hh
            return tuple(ms)
        return lax.fori_loop(0, n_chunks, step, carry)

    zero = jnp.zeros((1, 1), F32)
    carry = segment(zc_ref, grc_ref, hfc_ref, hbc_ref, t_x, t_c // ln, (zero,) * (2 * N_HEADS))
    segment(zx_ref, grx_ref, hfx_ref, hbx_ref, 0, t_x // ln, carry)

    def finish(hf_ref, hb_ref, z_ref, y_ref):
        blk = 256
        for h in range(N_HEADS):
            hc = slice(h * HEAD_DIM, (h + 1) * HEAD_DIM)
            for s in range(0, hf_ref.shape[0], blk):
                hs = hf_ref[s:s + blk, hc] + hb_ref[s:s + blk, hc]
                mu = jnp.mean(hs, axis=-1, keepdims=True)
                hd = hs - mu
                var = jnp.mean(hd * hd, axis=-1, keepdims=True)
                hn = hd * lax.rsqrt(var + EPS)
                o = z_ref[0, s:s + blk, o0 + h * HEAD_DIM:o0 + (h + 1) * HEAD_DIM].astype(F32)
                y_ref[0, s:s + blk, hc] = (hn * gn_ref[h] * _sigmoid(o)).astype(BF16)

    finish(hfx_ref, hbx_ref, zx_ref, yx_ref)
    if need_ctx:
        finish(hfc_ref, hbc_ref, zc_ref, yc_ref)


def _mlstm(qkvo_x, gr_x, qkvo_c, gr_c, wqk, gn, need_ctx):
    bsz, t_x, _ = qkvo_x.shape
    t_c = qkvo_c.shape[1]

    def seq_spec(t_len, w):
        return pl.BlockSpec((1, t_len, w), lambda b: (b, 0, 0))

    def gate_spec(t_len):
        return pl.BlockSpec((1, N_HEADS, t_len // CHUNK, GATE_ROWS, CHUNK), lambda b: (b, 0, 0, 0, 0))

    in_specs = [seq_spec(t_x, _W_QKVO), gate_spec(t_x), seq_spec(t_c, _W_QKVO), gate_spec(t_c),
                _const_spec(wqk.shape), _const_spec(gn.shape)]
    out_specs = [seq_spec(t_x, MLSTM_W)]
    out_shape = [jax.ShapeDtypeStruct((bsz, t_x, MLSTM_W), BF16)]
    if need_ctx:
        out_specs.append(seq_spec(t_c, MLSTM_W))
        out_shape.append(jax.ShapeDtypeStruct((bsz, t_c, MLSTM_W), BF16))
    scratch = [
        pltpu.VMEM((t_x + t_c, MLSTM_W), BF16), pltpu.VMEM((t_x + t_c, MLSTM_W), BF16),
        pltpu.VMEM((t_x, MLSTM_W), F32), pltpu.VMEM((t_x, MLSTM_W), F32),
        pltpu.VMEM((t_c, MLSTM_W), F32), pltpu.VMEM((t_c, MLSTM_W), F32),
        pltpu.VMEM((2 * N_HEADS, HEAD_DIM, 2 * HEAD_DIM), F32),
    ]
    outs = pl.pallas_call(
        functools.partial(_mlstm_body, need_ctx),
        grid=(bsz,),
        in_specs=in_specs, out_specs=out_specs, out_shape=out_shape, scratch_shapes=scratch,
        compiler_params=_cparams(("arbitrary",)),
        name="mlstm",
    )(qkvo_x, gr_x, qkvo_c, gr_c, wqk, gn)
    return (outs[0], outs[1]) if need_ctx else (outs[0], None)


def _dft_body(mc_ref, ms_ref, u_ref, y_ref):
    y = _dot(mc_ref[...], u_ref[0, :, :FOURIER_W]) + _dot(ms_ref[...], u_ref[0, :, FOURIER_W:])
    y_ref[0] = y.astype(BF16)


def _time_dft(u, mc, ms):
    bsz, t_len, _ = u.shape
    tm = min(t_len, 1024)
    return pl.pallas_call(
        _dft_body,
        grid=(t_len // tm, bsz),
        in_specs=[
            pl.BlockSpec((tm, t_len), lambda i, b: (i, 0)),
            pl.BlockSpec((tm, t_len), lambda i, b: (i, 0)),
            pl.BlockSpec((1, t_len, 2 * FOURIER_W), lambda i, b: (b, 0, 0)),
        ],
        out_specs=pl.BlockSpec((1, tm, FOURIER_W), lambda i, b: (b, i, 0)),
        out_shape=jax.ShapeDtypeStruct((bsz, t_len, FOURIER_W), BF16),
        compiler_params=_cparams(("arbitrary", "arbitrary")),
        name="time_dft",
    )(mc, ms, u)


_CONV_PAD = 16
_CONV_BLK = 128


def _conv_body(cz_ref, w_ref, b_ref, g_ref, bl_ref, y_ref, sh_ref):
    t_len = cz_ref.shape[1]
    n_sh = sh_ref.shape[1]
    sh_ref[0, 0:_CONV_PAD, :] = jnp.zeros((_CONV_PAD, CONV_W), F32)
    sh_ref[0, _CONV_PAD + t_len:n_sh, :] = jnp.zeros((n_sh - _CONV_PAD - t_len, CONV_W), F32)
    blk = 256
    for s in range(0, t_len, blk):
        a = cz_ref[0, s:s + blk, :CONV_W].astype(F32)
        gt = cz_ref[0, s:s + blk, CONV_W:].astype(F32)
        sh_ref[0, _CONV_PAD + s:_CONV_PAD + s + blk, :] = a * _sigmoid(gt)
    n_copy = n_sh - 8
    for r in range(1, 8):
        for s in range(0, n_copy, blk):
            n = min(blk, n_copy - s)
            sh_ref[r, s:s + n, :] = sh_ref[0, s + r:s + r + n, :]

    def block(i, _):
        t0 = pl.multiple_of(i * _CONV_BLK, _CONV_BLK)
        acc = jnp.zeros((_CONV_BLK, CONV_W), F32)
        for k in range(CONV_K):
            j = k + _CONV_PAD - CONV_K // 2
            acc = acc + w_ref[k:k + 1, :] * sh_ref[j % 8, pl.ds(t0 + (j // 8) * 8, _CONV_BLK), :]
        u = acc + b_ref[...]
        mu = jnp.mean(u, axis=-1, keepdims=True)
        uc = u - mu
        var = jnp.mean(uc * uc, axis=-1, keepdims=True)
        v = uc * lax.rsqrt(var + EPS) * g_ref[...] + bl_ref[...]
        y_ref[0, pl.ds(t0, _CONV_BLK), :] = (v * _sigmoid(v)).astype(BF16)
        return 0

    lax.fori_loop(0, t_len // _CONV_BLK, block, 0)


def _conv_module(cz, w_dw, b_dw, g_ln, b_ln):
    bsz, t_len, _ = cz.shape
    n_sh = t_len + 2 * _CONV_PAD + 8
    return pl.pallas_call(
        _conv_body,
        grid=(bsz,),
        in_specs=[
            pl.BlockSpec((1, t_len, 2 * CONV_W), lambda b: (b, 0, 0)),
            _const_spec((CONV_K, CONV_W)),
            _const_spec((1, CONV_W)), _const_spec((1, CONV_W)), _const_spec((1, CONV_W)),
        ],
        out_specs=pl.BlockSpec((1, t_len, CONV_W), lambda b: (b, 0, 0)),
        out_shape=jax.ShapeDtypeStruct((bsz, t_len, CONV_W), BF16),
        scratch_shapes=[pltpu.VMEM((8, n_sh, CONV_W), F32)],
        compiler_params=_cparams(("arbitrary",)),
        name="conv_module",
    )(cz, w_dw, b_dw, g_ln, b_ln)


_FF_BLK = 1024


def _post_body(x_ref, ym_ref, yf_ref, yv_ref, mod_ref, gpm_ref, gqm_ref, gpl_ref,
               wo_ref, w1_ref, w2_ref, o_ref):
    x = x_ref[0]
    o = (_dot(ym_ref[0], wo_ref[0:MLSTM_W, :])
         + _dot(yf_ref[0], wo_ref[MLSTM_W:MLSTM_W + FOURIER_W, :])
         + _dot(yv_ref[0], wo_ref[MLSTM_W + FOURIER_W:, :]))
    x = x + mod_ref[0, 2:3, :] * _rms(o, gpm_ref[...])
    h = (_rms(x, gqm_ref[...]) * (1.0 + mod_ref[0, 4:5, :]) + mod_ref[0, 3:4, :]).astype(BF16)
    acc = jnp.zeros(x.shape, F32)
    for c0 in range(0, D_FF, _FF_BLK):
        a = jnp.maximum(_dot(h, w1_ref[:, c0:c0 + _FF_BLK]), 0.0)
        acc = acc + _dot((a * a).astype(BF16), w2_ref[c0:c0 + _FF_BLK, :])
    o_ref[0] = x + mod_ref[0, 5:6, :] * _rms(acc, gpl_ref[...])


def _post(x, ym, yf, yv, mods, g_post_mix, g_pre_mlp, g_post_mlp, wo, w1, w2, tm):
    bk, tk, _ = x.shape
    row_spec = lambda w: pl.BlockSpec((1, tm, w), lambda b, i: (b, i, 0))
    return pl.pallas_call(
        _post_body,
        grid=(bk, tk // tm),
        in_specs=[
            row_spec(D_MODEL), row_spec(MLSTM_W), row_spec(FOURIER_W), row_spec(CONV_W),
            pl.BlockSpec((1, 6, D_MODEL), lambda b, i: (b, 0, 0)),
            _const_spec((1, D_MODEL)), _const_spec((1, D_MODEL)), _const_spec((1, D_MODEL)),
            _const_spec(wo.shape), _const_spec(w1.shape), _const_spec(w2.shape),
        ],
        out_specs=row_spec(D_MODEL),
        out_shape=jax.ShapeDtypeStruct((bk, tk, D_MODEL), F32),
        compiler_params=_cparams(("arbitrary", "arbitrary")),
        name="post_mlp",
    )(x, ym, yf, yv, mods, g_post_mix, g_pre_mlp, g_post_mlp, wo, w1, w2)


def _grid_sincos(rows, d):
    quarter = d // 4
    freq = jnp.exp(-math.log(POS_BASE) * jnp.arange(quarter, dtype=F32) / quarter)
    r = jnp.broadcast_to(jnp.arange(rows, dtype=F32)[:, None], (rows, GRID_W)).reshape(-1)
    col = jnp.broadcast_to(jnp.arange(GRID_W, dtype=F32)[None, :], (rows, GRID_W)).reshape(-1)
    ar = r[:, None] * freq
    ac = col[:, None] * freq
    return jnp.concatenate([jnp.sin(ar), jnp.cos(ar), jnp.sin(ac), jnp.cos(ac)], axis=-1)


def _dft_tables(n, scale):
    t = jnp.arange(n, dtype=jnp.int32)
    ang = ((t[:, None] * t[None, :]) % n).astype(F32) * (2.0 * math.pi / n)
    return (jnp.cos(ang) * scale).astype(BF16), (-jnp.sin(ang) * scale).astype(BF16)


def _channel_dft_table():
    c = jnp.arange(FOURIER_W, dtype=jnp.int32)
    same = (c[:, None] // FOURIER_HD) == (c[None, :] // FOURIER_HD)
    ang = (((c[:, None] % FOURIER_HD) * (c[None, :] % FOURIER_HD)) % FOURIER_HD).astype(F32) * (
        2.0 * math.pi / FOURIER_HD)
    cos = jnp.where(same, jnp.cos(ang), 0.0)
    sin = jnp.where(same, jnp.sin(ang), 0.0)
    return jnp.concatenate([cos, sin], axis=1).astype(BF16)


def _pack_in_weights(w_in, b_gate):
    wm = jnp.concatenate([w_in[:, :_W_QKVO], w_in[:, _F_OFF:_P_IN]], axis=1).astype(BF16)
    wg = w_in[:, _G_OFF:_F_OFF].reshape(D_MODEL, 4, N_HEADS).transpose(2, 1, 0)
    wg = jnp.pad(wg, ((0, 0), (0, GATE_ROWS - 4), (0, 0))).reshape(N_HEADS * GATE_ROWS, D_MODEL).astype(BF16)
    bg = jnp.pad(b_gate.T, ((0, 0), (0, GATE_ROWS - 4))).reshape(N_HEADS * GATE_ROWS, 1)
    return wm, wg, bg


def kernel(x, c, ctx, c_ctx, w_ada, b_ada, g_pre_mix, g_post_mix, g_pre_mlp, g_post_mlp, w_in, b_gate,
           w_qk_conv, g_mlstm_norm, w_dw, b_dw, g_conv_ln, b_conv_ln, w_out, w_mlp1, w_mlp2):
    bsz, t_x, d = x.shape
    t_c = ctx.shape[1]
    depth = w_ada.shape[0]

    pos = _grid_sincos(t_x // GRID_W, d)
    bd = _channel_dft_table()
    mcx, msx = _dft_tables(t_x, 1.0 / math.sqrt(t_x * FOURIER_HD))
    mcc, msc = _dft_tables(t_c, 1.0 / math.sqrt(t_c * FOURIER_HD))

    cc = jnp.zeros((16, d), F32).at[:bsz].set(c).at[bsz].set(c_ctx)
    mods = _ada(cc, w_ada, b_ada).reshape(depth, 16, 6, d)

    xs = x
    cs = ctx.reshape(1, bsz * t_c, d)
    for l in range(depth):
        need_ctx = l < depth - 1
        mods_x = mods[l, :bsz]
        mods_c = mods[l, bsz:bsz + 1]
        wm, wg, bg = _pack_in_weights(w_in[l], b_gate[l])
        row = lambda v: v[l].reshape(1, -1)

        outs = _in_proj(xs, pos if l == 0 else None, mods_x, row(g_pre_mix), wm, wg, bg, bd, ROW_TILE)
        if l == 0:
            xs, *outs = outs
        qkvo_x, u_x, cz_x, gr_x = outs
        qkvo_c, u_c, cz_c, gr_c = _in_proj(cs, None, mods_c, row(g_pre_mix), wm, wg, bg, bd, ROW_TILE)
        per_sample = lambda a: a.reshape((bsz, t_c) + a.shape[2:])
        qkvo_c, u_c, cz_c = per_sample(qkvo_c), per_sample(u_c), per_sample(cz_c)
        gr_c = gr_c.reshape(N_HEADS, bsz, t_c // CHUNK, GATE_ROWS, CHUNK).transpose(1, 0, 2, 3, 4)

        wqk = w_qk_conv[l].reshape(3, 2 * N_HEADS, HEAD_DIM).transpose(1, 0, 2)
        gn = g_mlstm_norm[l].reshape(N_HEADS, 1, HEAD_DIM)
        ym_x, ym_c = _mlstm(qkvo_x, gr_x, qkvo_c, gr_c, wqk, gn, need_ctx)

        conv_args = (w_dw[l], row(b_dw), row(g_conv_ln), row(b_conv_ln))
        yf_x = _time_dft(u_x, mcx, msx)
        yv_x = _conv_module(cz_x, *conv_args)
        post_w = (row(g_post_mix), row(g_pre_mlp), row(g_post_mlp),
                  w_out[l].astype(BF16), w_mlp1[l].astype(BF16), w_mlp2[l].astype(BF16))
        xs = _post(xs, ym_x, yf_x, yv_x, mods_x, *post_w, ROW_TILE)
        if need_ctx:
            yf_c = _time_dft(u_c, mcc, msc)
            yv_c = _conv_module(cz_c, *conv_args)
            flat = lambda a: a.reshape(1, bsz * t_c, a.shape[-1])
            cs = _post(cs, flat(ym_c), flat(yf_c), flat(yv_c), mods_c, *post_w, ROW_TILE)
    return xs
```
